```python
import math
import jax
import jax.numpy as jnp
from jax import lax
import numpy as np


D_MODEL = 1024
BATCH = 8
SEQ = 4096
DEPTH = 2

MEM_LEN = 256
MIX_W = 512
N_BRANCH = 3
S5_GROUP_CH = 16
S5_GROUPS = MIX_W // S5_GROUP_CH
S5_STATE = 64
GLA_HEADS = 4
GLA_DK = 64
GLA_DV = 128
GLA_GATE_RANK = 16
GLA_TAU = 16.0
GLA_CHUNK = 64
ATT_GROUPS = ((128, 1), (512, 4), (2048, 16))
ATT_HEADS_PER_GROUP = 4
N_ATT_HEADS = ATT_HEADS_PER_GROUP * len(ATT_GROUPS)
ATT_HEAD_DIM = MIX_W // ATT_HEADS_PER_GROUP
ATT_BLOCK = 128
ALIBI_MAX_EXP = 8.0
X_HEADS = 4
X_HEAD_DIM = D_MODEL // X_HEADS
D_FF = 4 * D_MODEL
RMS_EPS = 1e-6
PROJ_WIDTHS = (MIX_W,
               GLA_HEADS * GLA_DK,
               GLA_HEADS * GLA_DK,
               GLA_HEADS * GLA_DV,
               GLA_GATE_RANK,
               GLA_HEADS * GLA_DV,
               N_ATT_HEADS * ATT_HEAD_DIM,
               N_ATT_HEADS * ATT_HEAD_DIM,
               N_ATT_HEADS * ATT_HEAD_DIM,
               N_BRANCH * D_MODEL)
D_IN = sum(PROJ_WIDTHS)

kernel_name = 'hybrid_s5_gla_dilated_block'


def rms_norm(x, gain):
    xf = x.astype(jnp.float32)
    y = xf * lax.rsqrt(jnp.mean(xf * xf, axis=-1, keepdims=True) + RMS_EPS)
    return (y * gain.astype(jnp.float32)).astype(x.dtype)


def alibi_slopes(n):
    return 2.0 ** (-ALIBI_MAX_EXP * jnp.arange(1, n + 1, dtype=jnp.float32) / n)


def s5_branch(u, a_re, a_im, log_step, b_re, b_im, c_re, c_im, d_skip, w_glu, b_glu):
    f32 = jnp.float32
    bsz, seq, _ = u.shape
    uf = u.astype(f32)
    lam = lax.complex(a_re.astype(f32), a_im.astype(f32))
    step = jnp.exp(log_step.astype(f32))[:, None]
    lam_bar = jnp.exp(lam * step)
    b_bar = ((lam_bar - 1.0) / lam)[:, :, None] * lax.complex(b_re.astype(f32), b_im.astype(f32))
    ug = uf.reshape(bsz, seq, S5_GROUPS, S5_GROUP_CH).astype(jnp.complex64)
    bu = jnp.einsum('gpc,bsgc->bsgp', b_bar, ug)
    a_seq = jnp.broadcast_to(lam_bar, (1, seq) + lam_bar.shape)

    def combine(left, right):
        a_l, b_l = left
        a_r, b_r = right
        return a_r * a_l, a_r * b_l + b_r

    _, states = lax.associative_scan(combine, (a_seq, bu), axis=1)
    c_mat = lax.complex(c_re.astype(f32), c_im.astype(f32))
    y = jnp.einsum('gcp,bsgp->bsgc', c_mat, states).real.reshape(bsz, seq, MIX_W)
    y = y + d_skip.astype(f32) * uf
    g = jax.nn.gelu(y)
    return (g * jax.nn.sigmoid(g @ w_glu.astype(f32) + b_glu.astype(f32))).astype(u.dtype)


def gla_chunked(q, k, v, log_a):
    bsz, seq, nh, dk = q.shape
    dv = v.shape[-1]
    nc = seq // GLA_CHUNK

    def chunks(t):
        return t.reshape(bsz, nc, GLA_CHUNK, nh, t.shape[-1]).transpose(1, 0, 3, 2, 4)

    causal = jnp.tril(jnp.ones((GLA_CHUNK, GLA_CHUNK), dtype=bool))

    def step(state, inp):
        qc, kc, vc, lac = inp
        b = jnp.cumsum(lac, axis=2)
        inter = jnp.einsum('bhcd,bhde->bhce', qc * jnp.exp(b), state)
        diff = b[:, :, :, None, :] - b[:, :, None, :, :]
        decay = jnp.exp(jnp.where(causal[:, :, None], diff, -jnp.inf))
        scores = jnp.einsum('bhijd,bhjd->bhij', qc[:, :, :, None, :] * decay, kc)
        intra = jnp.einsum('bhij,bhje->bhie', scores, vc)
        b_end = b[:, :, -1:, :]
        new_state = jnp.exp(b_end[:, :, 0, :])[..., None] * state + jnp.einsum('bhjd,bhje->bhde', kc * jnp.exp(b_end - b), vc)
        return new_state, inter + intra

    state0 = jnp.zeros((bsz, nh, dk, dv), jnp.float32)
    _, out = lax.scan(step, state0, (chunks(q), chunks(k), chunks(v), chunks(log_a)))
    return out.transpose(1, 0, 3, 2, 4).reshape(bsz, seq, nh, dv)


def gla_branch(q, k, v, g_lr, r, w_gate, b_gate, g_out):
    f32 = jnp.float32
    bsz, seq, _ = q.shape
    qh = q.astype(f32).reshape(bsz, seq, GLA_HEADS, GLA_DK) * GLA_DK ** -0.5
    kh = k.astype(f32).reshape(bsz, seq, GLA_HEADS, GLA_DK)
    vh = v.astype(f32).reshape(bsz, seq, GLA_HEADS, GLA_DV)
    log_a = jax.nn.log_sigmoid(g_lr.astype(f32) @ w_gate.astype(f32) + b_gate.astype(f32)) / GLA_TAU
    log_a = log_a.reshape(bsz, seq, GLA_HEADS, GLA_DK)
    o = rms_norm(gla_chunked(qh, kh, vh, log_a), g_out)
    return (o.reshape(bsz, seq, MIX_W) * jax.nn.silu(r.astype(f32))).astype(q.dtype)


def dilated_window_attention(q, k, v, slopes, window, dilation):
    bsz, seq, hg, dh = q.shape
    sub_len = seq // dilation
    n_blk = -(-sub_len // ATT_BLOCK)
    pad_len = n_blk * ATT_BLOCK
    steps = window // dilation

    def to_sub(t):
        t = t.reshape(bsz, sub_len, dilation, hg, dh).transpose(0, 2, 1, 3, 4).reshape(bsz * dilation, sub_len, hg, dh)
        return jnp.pad(t, ((0, 0), (0, pad_len - sub_len), (0, 0), (0, 0)))

    def kv_blocks(t):
        tp = jnp.pad(t, ((0, 0), (ATT_BLOCK, 0), (0, 0), (0, 0)))
        prev = tp[:, :pad_len].reshape(-1, n_blk, ATT_BLOCK, hg, dh)
        cur = tp[:, ATT_BLOCK:].reshape(-1, n_blk, ATT_BLOCK, hg, dh)
        return jnp.concatenate([prev, cur], axis=2)

    qb = to_sub(q).reshape(-1, n_blk, ATT_BLOCK, hg, dh)
    kb = kv_blocks(to_sub(k))
    vb = kv_blocks(to_sub(v))
    scores = jnp.einsum('rnqhd,rnkhd->rnhqk', qb, kb) * dh ** -0.5
    blk = jnp.arange(n_blk)[:, None] * ATT_BLOCK
    q_pos = blk + jnp.arange(ATT_BLOCK)[None, :]
    k_pos = blk - ATT_BLOCK + jnp.arange(2 * ATT_BLOCK)[None, :]
    dist = q_pos[:, :, None] - k_pos[:, None, :]
    valid = (dist >= 0) & (dist <= steps) & (k_pos[:, None, :] >= 0)
    bias = -slopes[None, :, None, None] * (dilation * dist).astype(jnp.float32)[:, None]
    logits = jnp.where(valid[:, None], scores + bias, -jnp.inf)
    lse = jax.nn.logsumexp(logits, axis=-1)
    probs = jnp.exp(logits - lse[..., None])
    out = jnp.einsum('rnhqk,rnkhd->rnqhd', probs, vb).reshape(bsz, dilation, pad_len, hg, dh)[:, :, :sub_len]
    out = out.transpose(0, 2, 1, 3, 4).reshape(bsz, seq, hg, dh)
    lse = lse.transpose(0, 1, 3, 2).reshape(bsz, dilation, pad_len, hg)[:, :, :sub_len]
    lse = lse.transpose(0, 2, 1, 3).reshape(bsz, seq, hg)
    return out, lse


def dilated_branch(q, k, v):
    f32 = jnp.float32
    bsz, seq, _ = q.shape
    qh = q.astype(f32).reshape(bsz, seq, N_ATT_HEADS, ATT_HEAD_DIM)
    kh = k.astype(f32).reshape(bsz, seq, N_ATT_HEADS, ATT_HEAD_DIM)
    vh = v.astype(f32).reshape(bsz, seq, N_ATT_HEADS, ATT_HEAD_DIM)
    slopes = alibi_slopes(N_ATT_HEADS)
    outs, lses = [], []
    for g, (window, dilation) in enumerate(ATT_GROUPS):
        hs = slice(g * ATT_HEADS_PER_GROUP, (g + 1) * ATT_HEADS_PER_GROUP)
        o, lse = dilated_window_attention(qh[:, :, hs], kh[:, :, hs], vh[:, :, hs], slopes[hs], window, dilation)
        outs.append(o)
        lses.append(lse)
    outs = jnp.stack(outs, axis=2)
    weights = jax.nn.softmax(jnp.stack(lses, axis=2), axis=2)
    return jnp.sum(weights[..., None] * outs, axis=2).reshape(bsz, seq, MIX_W).astype(q.dtype)


def cross_attention(h, mem_n, w_q, w_kv, w_o):
    bsz, seq, _ = h.shape
    q = (h @ w_q).reshape(bsz, seq, X_HEADS, X_HEAD_DIM)
    k, v = jnp.split(mem_n @ w_kv, 2, axis=-1)
    k = k.reshape(bsz, MEM_LEN, X_HEADS, X_HEAD_DIM)
    v = v.reshape(bsz, MEM_LEN, X_HEADS, X_HEAD_DIM)
    scores = jnp.einsum('bshd,bmhd->bhsm', q, k).astype(jnp.float32) * X_HEAD_DIM ** -0.5
    probs = jax.nn.softmax(scores, axis=-1).astype(v.dtype)
    o = jnp.einsum('bhsm,bmhd->bshd', probs, v).reshape(bsz, seq, D_MODEL)
    return o @ w_o


def squared_relu_mlp(h, w_up, w_down):
    return jnp.square(jax.nn.relu(h @ w_up)) @ w_down


def setup_inputs(seed: int = 0) -> dict:
    key = jax.random.key(seed)
    ks = jax.random.split(key, 32)
    f32 = jnp.float32

    def nrm(k, shape, scale):
        return jax.random.normal(k, shape, f32) * scale

    def gain(k, shape):
        return 1.0 + 0.01 * jax.random.normal(k, shape, f32)

    return {
        'x': nrm(ks[0], (BATCH, SEQ, D_MODEL), 1.0),
        'mem': nrm(ks[1], (BATCH, MEM_LEN, D_MODEL), 1.0),
        'g_mix': gain(ks[2], (DEPTH, D_MODEL)),
        'w_in': nrm(ks[3], (DEPTH, D_MODEL, D_IN), D_MODEL ** -0.5),
        's5_a_re': -0.5 + nrm(ks[4], (DEPTH, S5_GROUPS, S5_STATE), 0.01),
        's5_a_im': math.pi * jnp.arange(S5_STATE, dtype=f32)[None, None, :] + nrm(ks[5], (DEPTH, S5_GROUPS, S5_STATE), 0.01),
        's5_log_step': jax.random.uniform(ks[6], (DEPTH, S5_GROUPS), f32, math.log(1e-3), math.log(1e-1)),
        's5_b_re': nrm(ks[7], (DEPTH, S5_GROUPS, S5_STATE, S5_GROUP_CH), (2 * S5_GROUP_CH) ** -0.5),
        's5_b_im': nrm(ks[8], (DEPTH, S5_GROUPS, S5_STATE, S5_GROUP_CH), (2 * S5_GROUP_CH) ** -0.5),
        's5_c_re': nrm(ks[9], (DEPTH, S5_GROUPS, S5_GROUP_CH, S5_STATE), S5_STATE ** -0.25),
        's5_c_im': nrm(ks[10], (DEPTH, S5_GROUPS, S5_GROUP_CH, S5_STATE), S5_STATE ** -0.25),
        's5_d': nrm(ks[11], (DEPTH, MIX_W), 1.0),
        'w_glu': nrm(ks[12], (DEPTH, MIX_W, MIX_W), MIX_W ** -0.5),
        'b_glu': nrm(ks[13], (DEPTH, MIX_W), 0.01),
        'w_gla_gate': nrm(ks[14], (DEPTH, GLA_GATE_RANK, GLA_HEADS * GLA_DK), GLA_GATE_RANK ** -0.5),
        'b_gla_gate': nrm(ks[15], (DEPTH, GLA_HEADS * GLA_DK), 0.01),
        'g_gla_out': gain(ks[16], (DEPTH, GLA_DV)),
        'w_branch': nrm(ks[17], (DEPTH, N_BRANCH, MIX_W, D_MODEL), MIX_W ** -0.5),
        'w_out': nrm(ks[18], (DEPTH, D_MODEL, D_MODEL), D_MODEL ** -0.5),
        'g_mem': gain(ks[19], (D_MODEL,)),
        'g_cross': gain(ks[20], (DEPTH, D_MODEL)),
        'w_xq': nrm(ks[21], (DEPTH, D_MODEL, D_MODEL), D_MODEL ** -0.5),
        'w_xkv': nrm(ks[22], (DEPTH, D_MODEL, 2 * D_MODEL), D_MODEL ** -0.5),
        'w_xo': nrm(ks[23], (DEPTH, D_MODEL, D_MODEL), D_MODEL ** -0.5),
        'g_mlp': gain(ks[24], (DEPTH, D_MODEL)),
        'w_up': nrm(ks[25], (DEPTH, D_MODEL, D_FF), D_MODEL ** -0.5),
        'w_down': nrm(ks[26], (DEPTH, D_FF, D_MODEL), D_FF ** -0.5),
        'g_final': gain(ks[27], (D_MODEL,)),
    }


def reference(x, mem, g_mix, w_in, s5_a_re, s5_a_im, s5_log_step, s5_b_re, s5_b_im, s5_c_re, s5_c_im, s5_d,
              w_glu, b_glu, w_gla_gate, b_gla_gate, g_gla_out, w_branch, w_out, g_mem, g_cross, w_xq, w_xkv,
              w_xo, g_mlp, w_up, w_down, g_final):
    bsz, seq, _ = x.shape
    mem_n = rms_norm(mem, g_mem)
    split_points = [int(p) for p in np.cumsum(PROJ_WIDTHS)[:-1]]
    for l in range(DEPTH):
        h = rms_norm(x, g_mix[l])
        (u_s5, q_gla, k_gla, v_gla, lr_gla, r_gla, q_att, k_att, v_att, gate_logits) = jnp.split(h @ w_in[l], split_points, axis=-1)
        y_a = s5_branch(u_s5, s5_a_re[l], s5_a_im[l], s5_log_step[l], s5_b_re[l], s5_b_im[l], s5_c_re[l], s5_c_im[l],
                        s5_d[l], w_glu[l], b_glu[l])
        y_b = gla_branch(q_gla, k_gla, v_gla, lr_gla, r_gla, w_gla_gate[l], b_gla_gate[l], g_gla_out[l])
        y_c = dilated_branch(q_att, k_att, v_att)
        branches = jnp.stack([y_a, y_b, y_c], axis=2)
        gates = jax.nn.sigmoid(gate_logits.reshape(bsz, seq, N_BRANCH, D_MODEL))
        merged = jnp.sum(gates * jnp.einsum('bsnc,ncd->bsnd', branches, w_branch[l]), axis=2)
        x = x + merged @ w_out[l]
        x = x + cross_attention(rms_norm(x, g_cross[l]), mem_n, w_xq[l], w_xkv[l], w_xo[l])
        x = x + squared_relu_mlp(rms_norm(x, g_mlp[l]), w_up[l], w_down[l])
    return rms_norm(x, g_final)
```

```python
import functools
import math

import jax
import jax.numpy as jnp
from jax import lax
from jax.experimental import pallas as pl
from jax.experimental.pallas import tpu as pltpu

F32 = jnp.float32
BF16 = jnp.bfloat16

D_MODEL = 1024
MEM_LEN = 256
MIX_W = 512
N_BRANCH = 3
S5_GROUP_CH = 16
S5_GROUPS = MIX_W // S5_GROUP_CH
S5_STATE = 64
S5_CHUNK = 16
GLA_HEADS = 4
GLA_DK = 64
GLA_DV = 128
GLA_GATE_RANK = 16
GLA_TAU = 16.0
GLA_CHUNK = 64
GLA_SUB = 16
ATT_GROUPS = ((128, 1), (512, 4), (2048, 16))
ATT_HEADS_PER_GROUP = 4
N_ATT_HEADS = ATT_HEADS_PER_GROUP * len(ATT_GROUPS)
ATT_HEAD_DIM = MIX_W // ATT_HEADS_PER_GROUP
ATT_BLOCK = 128
ALIBI_MAX_EXP = 8.0
X_HEADS = 4
X_HEAD_DIM = D_MODEL // X_HEADS
D_FF = 4 * D_MODEL
RMS_EPS = 1e-6
NEG_BIG = -1e30

COL_GATE = 0
COL_U = 3072
COL_GV = 3584
COL_GR = 4096
COL_AQ = 4608
COL_AK = 6144
COL_AV = 7680
COL_GQ = 9216
COL_GK = 9472
COL_LR = 9728
PROJ_W = 10240

VMEM_LIMIT = 56 * 1024 * 1024


def _cparams(sem):
    return pltpu.CompilerParams(dimension_semantics=sem, vmem_limit_bytes=VMEM_LIMIT)


def _rms(x, gain):
    return x * lax.rsqrt(jnp.mean(x * x, axis=-1, keepdims=True) + RMS_EPS) * gain


def _norm_matmul_kernel(x_ref, g_ref, w_ref, o_ref, xn_ref):
    @pl.when(pl.program_id(1) == 0)
    def _():
        xn_ref[...] = _rms(x_ref[...], g_ref[...]).astype(BF16)

    o_ref[...] = jnp.dot(xn_ref[...], w_ref[...], preferred_element_type=F32).astype(o_ref.dtype)


def _norm_matmul(x2d, gain, w, tm, tn):
    n, d = x2d.shape
    wp = w.shape[1]
    return pl.pallas_call(
        _norm_matmul_kernel,
        grid=(n // tm, wp // tn),
        in_specs=[pl.BlockSpec((tm, d), lambda i, j: (i, 0)),
                  pl.BlockSpec((1, d), lambda i, j: (0, 0)),
                  pl.BlockSpec((d, tn), lambda i, j: (0, j))],
        out_specs=pl.BlockSpec((tm, tn), lambda i, j: (i, j)),
        out_shape=jax.ShapeDtypeStruct((n, wp), BF16),
        scratch_shapes=[pltpu.VMEM((tm, d), BF16)],
        compiler_params=_cparams(("parallel", "arbitrary")),
        name="norm_proj",
    )(x2d, gain, w)


def _pack_w_in(w_in):
    widths = (MIX_W, GLA_HEADS * GLA_DK, GLA_HEADS * GLA_DK, GLA_HEADS * GLA_DV, GLA_GATE_RANK,
              GLA_HEADS * GLA_DV, N_ATT_HEADS * ATT_HEAD_DIM, N_ATT_HEADS * ATT_HEAD_DIM,
              N_ATT_HEADS * ATT_HEAD_DIM, N_BRANCH * D_MODEL)
    offs = [0]
    for wd in widths:
        offs.append(offs[-1] + wd)
    u, gq, gk, gv, lr, gr, aq, ak, av, gate = (w_in[:, offs[i]:offs[i + 1]] for i in range(10))
    pad = jnp.zeros((w_in.shape[0], PROJ_W - COL_LR - GLA_GATE_RANK), w_in.dtype)
    return jnp.concatenate([gate, u, gv, gr, aq, ak, av, gq, gk, lr, pad], axis=1).astype(BF16)


def _s5_prepare(a_re, a_im, log_step, b_re, b_im, c_re, c_im):
    lam = lax.complex(a_re.astype(F32), a_im.astype(F32))
    step = jnp.exp(log_step.astype(F32))[:, None]
    lam_dt = lam * step
    lam_bar = jnp.exp(lam_dt)
    b_bar = ((lam_bar - 1.0) / lam)[:, :, None] * lax.complex(b_re.astype(F32), b_im.astype(F32))
    c_mat = lax.complex(c_re.astype(F32), c_im.astype(F32))
    tau = jnp.arange(S5_CHUNK, dtype=F32)
    pw = jnp.exp(lam_dt[None] * tau[:, None, None])
    kern = jnp.einsum('gcp,tgp,gpd->gtcd', c_mat, pw, b_bar).real
    it = jnp.arange(S5_CHUNK)
    lag = it[:, None] - it[None, :]
    toep = jnp.where((lag >= 0)[None, :, :, None, None], kern[:, jnp.maximum(lag, 0)], 0.0)
    toep = toep.transpose(0, 1, 3, 2, 4).reshape(S5_GROUPS, S5_CHUNK * S5_GROUP_CH, S5_CHUNK * S5_GROUP_CH)
    m2 = pw[::-1].transpose(1, 2, 0)[:, :, :, None] * b_bar[:, :, None, :]
    m_in = jnp.concatenate([m2.real, m2.imag], axis=1).reshape(S5_GROUPS, 2 * S5_STATE, S5_CHUNK * S5_GROUP_CH)
    pw1 = jnp.exp(lam_dt[None] * (tau + 1.0)[:, None, None]).transpose(1, 0, 2)
    m1 = c_mat[:, None, :, :] * pw1[:, :, None, :]
    m_out = jnp.concatenate([m1.real, -m1.imag], axis=-1).reshape(S5_GROUPS, S5_CHUNK * S5_GROUP_CH, 2 * S5_STATE)
    a16 = jnp.exp(lam_dt * float(S5_CHUNK)).reshape(S5_GROUPS * S5_STATE, 1)
    a16_re = jnp.broadcast_to(a16.real, (S5_GROUPS * S5_STATE, 128))
    a16_im = jnp.broadcast_to(a16.imag, (S5_GROUPS * S5_STATE, 128))
    return toep.astype(BF16), m_in.astype(BF16), m_out.astype(BF16), a16_re.astype(F32), a16_im.astype(F32)


def _s5_kernel(*refs, n_chunks):
    u_refs = refs[:S5_CHUNK]
    (toep_ref, min_ref, mout_ref, are_ref, aim_ref, dcol_ref, wglu_ref, bglu_ref,
     o_ref, ut_ref, yt_ref) = refs[S5_CHUNK:]
    nc = n_chunks
    rep = nc // 128
    gc = S5_GROUP_CH
    blk = S5_CHUNK * gc

    for j in range(S5_CHUNK):
        ut_ref[j] = u_refs[j][0].astype(F32).T.astype(BF16)

    lane = lax.broadcasted_iota(jnp.int32, (S5_STATE, nc), 1)
    n_steps = int(math.log2(nc))

    def group_body(g, carry):
        row0 = pl.multiple_of(g * gc, gc)
        u2t = ut_ref[:, pl.ds(row0, gc), :].reshape(blk, nc)
        s_loc = jnp.dot(min_ref[g], u2t, preferred_element_type=F32)
        s_re, s_im = s_loc[:S5_STATE], s_loc[S5_STATE:]
        prow = pl.multiple_of(g * S5_STATE, S5_STATE)
        a_re = jnp.tile(are_ref[pl.ds(prow, S5_STATE), :], (1, rep))
        a_im = jnp.tile(aim_ref[pl.ds(prow, S5_STATE), :], (1, rep))
        for k in range(n_steps):
            sh = 1 << k
            keep = lane >= sh
            p_re = jnp.where(keep, pltpu.roll(s_re, sh, 1), 0.0)
            p_im = jnp.where(keep, pltpu.roll(s_im, sh, 1), 0.0)
            s_re, s_im = (s_re + a_re * p_re - a_im * p_im,
                          s_im + a_re * p_im + a_im * p_re)
            if k + 1 < n_steps:
                a_re, a_im = a_re * a_re - a_im * a_im, 2.0 * a_re * a_im
        first = lane >= 1
        e_re = jnp.where(first, pltpu.roll(s_re, 1, 1), 0.0)
        e_im = jnp.where(first, pltpu.roll(s_im, 1, 1), 0.0)
        s_in = jnp.concatenate([e_re, e_im], axis=0).astype(BF16)
        y2t = (jnp.dot(toep_ref[g], u2t, preferred_element_type=F32)
               + jnp.dot(mout_ref[g], s_in, preferred_element_type=F32))
        yt_ref[:, pl.ds(row0, gc), :] = y2t.reshape(S5_CHUNK, gc, nc)
        return carry

    lax.fori_loop(0, S5_GROUPS, group_body, 0)

    dcol = jnp.tile(dcol_ref[...], (1, rep))
    for i in range(S5_CHUNK):
        y = (yt_ref[i] + dcol * ut_ref[i].astype(F32)).T
        gl = jax.nn.gelu(y, approximate=True)
        z = jnp.dot(gl.astype(BF16), wglu_ref[...], preferred_element_type=F32) + bglu_ref[...]
        o_ref[0, :, i * MIX_W:(i + 1) * MIX_W] = (gl * jax.nn.sigmoid(z)).astype(o_ref.dtype)


def _s5_branch(proj, bsz, seq, prep, d_skip, w_glu, b_glu):
    toep, m_in, m_out, a_re, a_im = prep
    nc = seq // S5_CHUNK
    pv = proj.reshape(bsz, nc, S5_CHUNK * PROJ_W)
    u_specs = [pl.BlockSpec((1, nc, MIX_W), functools.partial(
        lambda b, j: (b, 0, j * (PROJ_W // MIX_W) + COL_U // MIX_W), j=j)) for j in range(S5_CHUNK)]
    full = lambda shape: pl.BlockSpec(shape, lambda b: (0,) * len(shape))
    dcol = jnp.broadcast_to(d_skip.astype(F32)[:, None], (MIX_W, 128))
    out = pl.pallas_call(
        functools.partial(_s5_kernel, n_chunks=nc),
        grid=(bsz,),
        in_specs=u_specs + [full(toep.shape), full(m_in.shape), full(m_out.shape), full(a_re.shape),
                            full(a_im.shape), full(dcol.shape), full((MIX_W, MIX_W)), full((1, MIX_W))],
        out_specs=pl.BlockSpec((1, nc, S5_CHUNK * MIX_W), lambda b: (b, 0, 0)),
        out_shape=jax.ShapeDtypeStruct((bsz, nc, S5_CHUNK * MIX_W), BF16),
        scratch_shapes=[pltpu.VMEM((S5_CHUNK, MIX_W, nc), BF16),
                        pltpu.VMEM((S5_CHUNK, MIX_W, nc), F32)],
        compiler_params=_cparams(("parallel",)),
        name="s5_branch",
    )(*([pv] * S5_CHUNK), toep, m_in, m_out, a_re, a_im, dcol,
      w_glu.astype(BF16), b_glu.astype(F32).reshape(1, MIX_W))
    return out.reshape(bsz * seq, MIX_W)


def _log_sigmoid(x):
    return jnp.minimum(x, 0.0) - jnp.log1p(jnp.exp(-jnp.abs(x)))


def _gla_kernel(q_ref, k_ref, v_ref, r_ref, lr_ref, wg_ref, bg_ref, gout_ref, o_ref, st_ref, *, n_chunks):
    hd = GLA_HEADS * GLA_DK
    hv = GLA_HEADS * GLA_DV
    ch = GLA_CHUNK

    @pl.when(pl.program_id(1) == 0)
    def _():
        st_ref[...] = jnp.zeros_like(st_ref)

    ri = lax.broadcasted_iota(jnp.int32, (ch, ch), 0)
    ci = lax.broadcasted_iota(jnp.int32, (ch, ch), 1)
    tril = jnp.where(ri >= ci, 1.0, 0.0).astype(BF16)
    kh_r = lax.broadcasted_iota(jnp.int32, (GLA_HEADS * ch, hd), 0) // ch
    kh_c = lax.broadcasted_iota(jnp.int32, (GLA_HEADS * ch, hd), 1) // GLA_DK
    kmask = kh_r == kh_c
    vh_r = lax.broadcasted_iota(jnp.int32, (GLA_HEADS * ch, hv), 0) // ch
    vh_c = lax.broadcasted_iota(jnp.int32, (GLA_HEADS * ch, hv), 1) // GLA_DV
    vmask = vh_r == vh_c
    sh_r = lax.broadcasted_iota(jnp.int32, (hv, hd), 0) // GLA_DV
    sh_c = lax.broadcasted_iota(jnp.int32, (hv, hd), 1) // GLA_DK
    smask = sh_r == sh_c
    si = lax.broadcasted_iota(jnp.int32, (ch, GLA_HEADS * ch), 0)
    sj = lax.broadcasted_iota(jnp.int32, (ch, GLA_HEADS * ch), 1) % ch
    causal = sj <= si
    krow = lax.broadcasted_iota(jnp.int32, (ch, hd), 0)

    def chunk_body(c, carry):
        sl = pl.ds(pl.multiple_of(c * ch, ch), ch)
        q = q_ref[0, sl, :].astype(F32) * (GLA_DK ** -0.5)
        k = k_ref[0, sl, :].astype(F32)
        v = v_ref[0, sl, :]
        x = jnp.dot(lr_ref[0, sl, :], wg_ref[...], preferred_element_type=F32) + bg_ref[...]
        la = _log_sigmoid(x) * (1.0 / GLA_TAU)
        la_hi = la.astype(BF16)
        la_lo = (la - la_hi.astype(F32)).astype(BF16)
        b = (jnp.dot(tril, la_hi, preferred_element_type=F32)
             + jnp.dot(tril, la_lo, preferred_element_type=F32))
        b_end = b[ch - 1:ch, :]
        st = st_ref[...]
        inter = lax.dot_general((q * jnp.exp(b)).astype(BF16), st.astype(BF16),
                                (((1,), (1,)), ((), ())), preferred_element_type=F32)
        rows = []
        for blk in range(ch // GLA_SUB):
            lo = blk * GLA_SUB
            ref = b[lo:lo + 1, :]
            qt = q[lo:lo + GLA_SUB] * jnp.exp(b[lo:lo + GLA_SUB] - ref)
            kt = k * jnp.exp(jnp.where(krow < lo + GLA_SUB, ref - b, 0.0))
            kbd = jnp.where(kmask, jnp.tile(kt, (GLA_HEADS, 1)), 0.0).astype(BF16)
            rows.append(lax.dot_general(qt.astype(BF16), kbd, (((1,), (1,)), ((), ())),
                                        preferred_element_type=F32))
        scores = jnp.where(causal, jnp.concatenate(rows, axis=0), 0.0)
        vbd = jnp.where(vmask, jnp.tile(v, (GLA_HEADS, 1)), jnp.zeros((), BF16))
        intra = jnp.dot(scores.astype(BF16), vbd, preferred_element_type=F32)
        k_dec = (k * jnp.exp(b_end - b)).astype(BF16)
        upd = lax.dot_general(v, k_dec, (((0,), (0,)), ((), ())), preferred_element_type=F32)
        st_ref[...] = st * jnp.exp(b_end) + jnp.where(smask, upd, 0.0)
        o = inter + intra
        outs = []
        for h in range(GLA_HEADS):
            oh = o[:, h * GLA_DV:(h + 1) * GLA_DV]
            outs.append(oh * lax.rsqrt(jnp.mean(oh * oh, axis=-1, keepdims=True) + RMS_EPS))
        on = jnp.concatenate(outs, axis=1) * gout_ref[...]
        r = r_ref[0, sl, :].astype(F32)
        o_ref[0, sl, :] = (on * (r * jax.nn.sigmoid(r))).astype(o_ref.dtype)
        return carry

    lax.fori_loop(0, n_chunks, chunk_body, 0)


def _gla_branch(proj, bsz, seq, w_gate, b_gate, g_out, tc):
    p3 = proj.reshape(bsz, seq, PROJ_W)
    hd = GLA_HEADS * GLA_DK
    hv = GLA_HEADS * GLA_DV
    wg = jnp.zeros((128, hd), F32).at[:GLA_GATE_RANK].set(w_gate.astype(F32)).astype(BF16)
    bg = b_gate.astype(F32).reshape(1, hd)
    gout = jnp.tile(g_out.astype(F32), GLA_HEADS).reshape(1, hv)
    col = lambda blk: (lambda b, t: (b, t, blk))
    const = lambda shape: pl.BlockSpec(shape, lambda b, t: (0, 0))
    out = pl.pallas_call(
        functools.partial(_gla_kernel, n_chunks=tc // GLA_CHUNK),
        grid=(bsz, seq // tc),
        in_specs=[pl.BlockSpec((1, tc, hd), col(COL_GQ // hd)),
                  pl.BlockSpec((1, tc, hd), col(COL_GK // hd)),
                  pl.BlockSpec((1, tc, hv), col(COL_GV // hv)),
                  pl.BlockSpec((1, tc, hv), col(COL_GR // hv)),
                  pl.BlockSpec((1, tc, 128), col(COL_LR // 128)),
                  const((128, hd)), const((1, hd)), const((1, hv))],
        out_specs=pl.BlockSpec((1, tc, hv), lambda b, t: (b, t, 0)),
        out_shape=jax.ShapeDtypeStruct((bsz, seq, hv), BF16),
        scratch_shapes=[pltpu.VMEM((hv, hd), F32)],
        compiler_params=_cparams(("parallel", "arbitrary")),
        name="gla_branch",
    )(p3, p3, p3, p3, p3, wg, bg, gout)
    return out.reshape(bsz * seq, hv)


def _dilated_kernel(q_ref, kc_ref, vc_ref, kp_ref, vp_ref, o_ref, lse_ref, *, dilation, slopes, n_blk):
    blk = ATT_BLOCK
    dh = ATT_HEAD_DIM
    not_first_tile = pl.program_id(2) > 0
    qi = lax.broadcasted_iota(jnp.int32, (blk, blk), 0)
    kj = lax.broadcasted_iota(jnp.int32, (blk, blk), 1)
    dist_cur = (qi - kj).astype(F32)
    dist_prev = dist_cur + float(blk)
    valid_cur = kj <= qi
    tri_prev = kj >= qi
    lane = lax.broadcasted_iota(jnp.int32, (blk, 128), 1)
    scale = dh ** -0.5
    for i in range(n_blk):
        rows = slice(i * blk, (i + 1) * blk)
        lse_tile = jnp.zeros((blk, 128), F32)
        for h in range(ATT_HEADS_PER_GROUP):
            cols = slice(h * dh, (h + 1) * dh)
            q = q_ref[0, rows, cols]
            k_c = kc_ref[0, rows, cols]
            v_c = vc_ref[0, rows, cols]
            if i == 0:
                k_p, v_p = kp_ref[0, :, cols], vp_ref[0, :, cols]
                valid_prev = jnp.logical_and(tri_prev, not_first_tile)
            else:
                prow = slice((i - 1) * blk, i * blk)
                k_p, v_p = kc_ref[0, prow, cols], vc_ref[0, prow, cols]
                valid_prev = tri_prev
            nt = (((1,), (1,)), ((), ()))
            bias = slopes[h] * float(dilation)
            s_c = lax.dot_general(q, k_c, nt, preferred_element_type=F32) * scale - bias * dist_cur
            s_p = lax.dot_general(q, k_p, nt, preferred_element_type=F32) * scale - bias * dist_prev
            s_c = jnp.where(valid_cur, s_c, NEG_BIG)
            s_p = jnp.where(valid_prev, s_p, NEG_BIG)
            m = jnp.maximum(jnp.max(s_c, axis=-1, keepdims=True), jnp.max(s_p, axis=-1, keepdims=True))
            p_c = jnp.exp(s_c - m)
            p_p = jnp.exp(s_p - m)
            l = jnp.sum(p_c, axis=-1, keepdims=True) + jnp.sum(p_p, axis=-1, keepdims=True)
            acc = (jnp.dot(p_c.astype(BF16), v_c, preferred_element_type=F32)
                   + jnp.dot(p_p.astype(BF16), v_p, preferred_element_type=F32))
            o_ref[0, rows, cols] = (acc / l).astype(o_ref.dtype)
            lse_tile = jnp.where(lane == h, m + jnp.log(l), lse_tile)
        lse_ref[0, rows, :] = lse_tile


def _alibi_slopes():
    return [2.0 ** (-ALIBI_MAX_EXP * (i + 1) / N_ATT_HEADS) for i in range(N_ATT_HEADS)]


def _dilated_group(proj, bsz, seq, group):
    _, dil = ATT_GROUPS[group]
    sub = seq // dil
    tq = min(512, sub)
    n_blk = tq // ATT_BLOCK
    pv = proj.reshape(bsz, sub, dil * PROJ_W)
    per_row = PROJ_W // MIX_W
    qb, kb, vb = (c // MIX_W + group for c in (COL_AQ, COL_AK, COL_AV))
    cur = lambda base: (lambda b, r, m: (b, m, r * per_row + base))
    prev = lambda base: (lambda b, r, m: (b, jnp.maximum(m * n_blk - 1, 0), r * per_row + base))
    slopes = _alibi_slopes()[group * ATT_HEADS_PER_GROUP:(group + 1) * ATT_HEADS_PER_GROUP]
    o, lse = pl.pallas_call(
        functools.partial(_dilated_kernel, dilation=dil, slopes=slopes, n_blk=n_blk),
        grid=(bsz, dil, sub // tq),
        in_specs=[pl.BlockSpec((1, tq, MIX_W), cur(qb)),
                  pl.BlockSpec((1, tq, MIX_W), cur(kb)),
                  pl.BlockSpec((1, tq, MIX_W), cur(vb)),
                  pl.BlockSpec((1, ATT_BLOCK, MIX_W), prev(kb)),
                  pl.BlockSpec((1, ATT_BLOCK, MIX_W), prev(vb))],
        out_specs=[pl.BlockSpec((1, tq, MIX_W), lambda b, r, m: (b, m, r)),
                   pl.BlockSpec((1, tq, 128), lambda b, r, m: (b, m, r))],
        out_shape=[jax.ShapeDtypeStruct((bsz, sub, dil * MIX_W), BF16),
                   jax.ShapeDtypeStruct((bsz, sub, dil * 128), F32)],
        compiler_params=_cparams(("parallel", "parallel", "arbitrary")),
        name=f"dilated_attn_g{group}",
    )(pv, pv, pv, pv, pv)
    return o.reshape(bsz * seq, MIX_W), lse.reshape(bsz * seq, 128)


def _merge_kernel(x_ref, gate_ref, ya_ref, yb_ref, o0_ref, o1_ref, o2_ref, l0_ref, l1_ref, l2_ref,
                  wb_ref, wo_ref, out_ref):
    l0, l1, l2 = l0_ref[...], l1_ref[...], l2_ref[...]
    mx = jnp.maximum(jnp.maximum(l0, l1), l2)
    e0, e1, e2 = jnp.exp(l0 - mx), jnp.exp(l1 - mx), jnp.exp(l2 - mx)
    inv = 1.0 / (e0 + e1 + e2)
    ws = (e0 * inv, e1 * inv, e2 * inv)
    os_ = (o0_ref, o1_ref, o2_ref)
    tm = x_ref.shape[0]
    parts = []
    for h in range(ATT_HEADS_PER_GROUP):
        cols = slice(h * ATT_HEAD_DIM, (h + 1) * ATT_HEAD_DIM)
        acc = jnp.zeros((tm, ATT_HEAD_DIM), F32)
        for g in range(len(ATT_GROUPS)):
            wcol = jnp.broadcast_to(ws[g][:, h:h + 1], (tm, ATT_HEAD_DIM))
            acc = acc + wcol * os_[g][:, cols].astype(F32)
        parts.append(acc)
    yc = jnp.concatenate(parts, axis=1).astype(BF16)
    branches = (ya_ref[...], yb_ref[...], yc)
    merged = jnp.zeros((tm, D_MODEL), F32)
    for n in range(N_BRANCH):
        gate = jax.nn.sigmoid(gate_ref[:, n * D_MODEL:(n + 1) * D_MODEL].astype(F32))
        merged = merged + gate * jnp.dot(branches[n], wb_ref[n], preferred_element_type=F32)
    out_ref[...] = x_ref[...] + jnp.dot(merged.astype(BF16), wo_ref[...], preferred_element_type=F32)


def _merge(x2d, proj, ya, yb, atts, w_branch, w_out, tm):
    n = x2d.shape[0]
    row = lambda w: pl.BlockSpec((tm, w), lambda i: (i, 0))
    (o0, l0), (o1, l1), (o2, l2) = atts
    return pl.pallas_call(
        _merge_kernel,
        grid=(n // tm,),
        in_specs=[row(D_MODEL), row(N_BRANCH * D_MODEL), row(MIX_W), row(MIX_W),
                  row(MIX_W), row(MIX_W), row(MIX_W), row(128), row(128), row(128),
                  pl.BlockSpec((N_BRANCH, MIX_W, D_MODEL), lambda i: (0, 0, 0)),
                  pl.BlockSpec((D_MODEL, D_MODEL), lambda i: (0, 0))],
        out_specs=row(D_MODEL),
        out_shape=jax.ShapeDtypeStruct((n, D_MODEL), F32),
        compiler_params=_cparams(("parallel",)),
        name="merge_out_proj",
    )(x2d, proj, ya, yb, o0, o1, o2, l0, l1, l2, w_branch.astype(BF16), w_out.astype(BF16))


def _mem_kv_kernel(mem_ref, g_ref, w_ref, k_ref, v_ref):
    mn = _rms(mem_ref[0], g_ref[...]).astype(BF16)
    kv = jnp.dot(mn, w_ref[...], preferred_element_type=F32)
    k_ref[0] = kv[:, :D_MODEL].astype(BF16)
    v_ref[0] = kv[:, D_MODEL:].astype(BF16)


def _mem_kv(mem, g_mem, w_xkv):
    bsz = mem.shape[0]
    blk = pl.BlockSpec((1, MEM_LEN, D_MODEL), lambda b: (b, 0, 0))
    return pl.pallas_call(
        _mem_kv_kernel,
        grid=(bsz,),
        in_specs=[blk, pl.BlockSpec((1, D_MODEL), lambda b: (0, 0)),
                  pl.BlockSpec((D_MODEL, 2 * D_MODEL), lambda b: (0, 0))],
        out_specs=[blk, blk],
        out_shape=[jax.ShapeDtypeStruct((bsz, MEM_LEN, D_MODEL), BF16)] * 2,
        compiler_params=_cparams(("parallel",)),
        name="mem_kv_proj",
    )(mem, g_mem.astype(F32).reshape(1, D_MODEL), w_xkv.astype(BF16))


def _cross_kernel(x_ref, g_ref, k_ref, v_ref, wq_ref, wo_ref, out_ref):
    x = x_ref[0]
    h = _rms(x, g_ref[...]).astype(BF16)
    q = jnp.dot(h, wq_ref[...], preferred_element_type=F32).astype(BF16)
    heads = []
    for hh in range(X_HEADS):
        cols = slice(hh * X_HEAD_DIM, (hh + 1) * X_HEAD_DIM)
        s = lax.dot_general(q[:, cols], k_ref[0, :, cols], (((1,), (1,)), ((), ())),
                            preferred_element_type=F32) * (X_HEAD_DIM ** -0.5)
        p = jnp.exp(s - jnp.max(s, axis=-1, keepdims=True))
        p = p / jnp.sum(p, axis=-1, keepdims=True)
        heads.append(jnp.dot(p.astype(BF16), v_ref[0, :, cols], preferred_element_type=F32))
    o = jnp.concatenate(heads, axis=1).astype(BF16)
    out_ref[0] = x + jnp.dot(o, wo_ref[...], preferred_element_type=F32)


def _cross(x3d, g_cross, kmem, vmem, w_xq, w_xo, tm):
    bsz, seq, _ = x3d.shape
    xb = pl.BlockSpec((1, tm, D_MODEL), lambda b, t: (b, t, 0))
    mb = pl.BlockSpec((1, MEM_LEN, D_MODEL), lambda b, t: (b, 0, 0))
    wb = pl.BlockSpec((D_MODEL, D_MODEL), lambda b, t: (0, 0))
    return pl.pallas_call(
        _cross_kernel,
        grid=(bsz, seq // tm),
        in_specs=[xb, pl.BlockSpec((1, D_MODEL), lambda b, t: (0, 0)), mb, mb, wb, wb],
        out_specs=xb,
        out_shape=jax.ShapeDtypeStruct(x3d.shape, F32),
        compiler_params=_cparams(("parallel", "parallel")),
        name="cross_attn",
    )(x3d, g_cross.astype(F32).reshape(1, D_MODEL), kmem, vmem, w_xq.astype(BF16), w_xo.astype(BF16))


def _mlp_kernel(x_ref, g_ref, wu_ref, wd_ref, gf_ref, out_ref, *, final_norm, ff_chunk):
    x = x_ref[...]
    h = _rms(x, g_ref[...]).astype(BF16)
    acc = x
    for c in range(D_FF // ff_chunk):
        cols = slice(c * ff_chunk, (c + 1) * ff_chunk)
        up = jnp.maximum(jnp.dot(h, wu_ref[:, cols], preferred_element_type=F32), 0.0)
        acc = acc + jnp.dot((up * up).astype(BF16), wd_ref[cols, :], preferred_element_type=F32)
    if final_norm:
        acc = _rms(acc, gf_ref[...])
    out_ref[...] = acc


def _mlp(x2d, g_mlp, w_up, w_down, g_final, final_norm, tm):
    n = x2d.shape[0]
    row = pl.BlockSpec((tm, D_MODEL), lambda i: (i, 0))
    vec = pl.BlockSpec((1, D_MODEL), lambda i: (0, 0))
    once = pl.Buffered(1)
    return pl.pallas_call(
        functools.partial(_mlp_kernel, final_norm=final_norm, ff_chunk=1024),
        grid=(n // tm,),
        in_specs=[row, vec,
                  pl.BlockSpec((D_MODEL, D_FF), lambda i: (0, 0), pipeline_mode=once),
                  pl.BlockSpec((D_FF, D_MODEL), lambda i: (0, 0), pipeline_mode=once),
                  vec],
        out_specs=row,
        out_shape=jax.ShapeDtypeStruct((n, D_MODEL), F32),
        compiler_params=_cparams(("parallel",)),
        name="mlp_final" if final_norm else "mlp",
    )(x2d, g_mlp.astype(F32).reshape(1, D_MODEL), w_up.astype(BF16), w_down.astype(BF16),
      g_final.astype(F32).reshape(1, D_MODEL))


def kernel(x, mem, g_mix, w_in, s5_a_re, s5_a_im, s5_log_step, s5_b_re, s5_b_im, s5_c_re, s5_c_im, s5_d, w_glu, b_glu, w_gla_gate, b_gla_gate, g_gla_out, w_branch, w_out, g_mem, g_cross, w_xq, w_xkv, w_xo, g_mlp, w_up, w_down, g_final):
    bsz, seq, _ = x.shape
    n = bsz * seq
    depth = w_in.shape[0]
    assert seq % (ATT_BLOCK * ATT_GROUPS[-1][1]) == 0
    x2d = x.astype(F32).reshape(n, D_MODEL)
    mem = mem.astype(F32)
    for l in range(depth):
        proj = _norm_matmul(x2d, g_mix[l].astype(F32).reshape(1, D_MODEL), _pack_w_in(w_in[l]), 1024, 512)
        prep = _s5_prepare(s5_a_re[l], s5_a_im[l], s5_log_step[l], s5_b_re[l], s5_b_im[l], s5_c_re[l], s5_c_im[l])
        ya = _s5_branch(proj, bsz, seq, prep, s5_d[l], w_glu[l], b_glu[l])
        yb = _gla_branch(proj, bsz, seq, w_gla_gate[l], b_gla_gate[l], g_gla_out[l], 512)
        atts = [_dilated_group(proj, bsz, seq, g) for g in range(len(ATT_GROUPS))]
        x2d = _merge(x2d, proj, ya, yb, atts, w_branch[l], w_out[l], 512)
        kmem, vmem = _mem_kv(mem, g_mem, w_xkv[l])
        x2d = _cross(x2d.reshape(bsz, seq, D_MODEL), g_cross[l], kmem, vmem, w_xq[l], w_xo[l], 512).reshape(n, D_MODEL)
        x2d = _mlp(x2d, g_mlp[l], w_up[l], w_down[l], g_final, l == depth - 1, 512)
    return x2d.reshape(bsz, seq, D_MODEL).astype(x.dtype)
```
